```python
import jax, jax.numpy as jnp
from jax import lax
import numpy as np

D_MODEL = 1024
BATCH = 8
SEQ = 2048
DEPTH = 2
DEC_BATCH = 32
DEC_SEQ = 32
PAST_LEN = 2048

CHUNK = 64
GLA_HEADS = 4
GLA_DK = 64
GLA_DV = 128
GLA_QK = GLA_HEADS * GLA_DK
GLA_WIDTH = GLA_HEADS * GLA_DV
GLA_RANK = 16
GLA_TAU = 16.0
RG_WIDTH = D_MODEL - GLA_WIDTH
RG_BLOCKS = 8
RG_BLOCK = RG_WIDTH // RG_BLOCKS
RG_C = 8.0
CONV_W = 4
D_FF = 4 * D_MODEL
EPS = 1e-6
OFF_Q = 0
OFF_K = OFF_Q + GLA_QK
OFF_V = OFF_K + GLA_QK
OFF_G = OFF_V + GLA_WIDTH
OFF_LR = OFF_G + GLA_WIDTH
OFF_XR = OFF_LR + GLA_RANK
OFF_GR = OFF_XR + RG_WIDTH
D_IN = OFF_GR + RG_WIDTH

kernel_name = 'hymba_gla_rglru_streaming_step'


def rmsnorm(x, g):
    xf = x.astype(jnp.float32)
    y = xf * lax.rsqrt(jnp.mean(xf * xf, axis=-1, keepdims=True) + EPS) * g.astype(jnp.float32)
    return y.astype(x.dtype)


def gla_mix(q, k, v, log_a, S0):
    B, T = q.shape[0], q.shape[1]
    C = min(CHUNK, T)
    n = T // C

    def to_chunks(t):
        return t.reshape(B, n, C, GLA_HEADS, t.shape[-1]).transpose(1, 0, 3, 2, 4)

    causal = jnp.tril(jnp.ones((C, C), dtype=bool))[None, None, :, :, None]

    def step(S, inp):
        qc, kc, vc, lc = inp
        b = jnp.cumsum(lc, axis=2)
        o_inter = jnp.einsum('bhcd,bhde->bhce', qc * jnp.exp(b), S)
        diff = jnp.where(causal, b[:, :, :, None, :] - b[:, :, None, :, :], -jnp.inf)
        att = jnp.einsum('bhid,bhjd,bhijd->bhij', qc, kc, jnp.exp(diff))
        o = o_inter + jnp.einsum('bhij,bhje->bhie', att, vc)
        b_last = b[:, :, -1:, :]
        S_new = jnp.exp(b_last[:, :, 0, :])[..., None] * S + jnp.einsum(
            'bhcd,bhce->bhde', kc * jnp.exp(b_last - b), vc)
        return S_new, o

    S, o = lax.scan(step, S0, (to_chunks(q), to_chunks(k), to_chunks(v), to_chunks(log_a)))
    o = o.transpose(1, 0, 3, 2, 4).reshape(B, T, GLA_HEADS, GLA_DV)
    return o, S


def causal_conv(x, buf, w, b):
    T = x.shape[1]
    xp = jnp.concatenate([buf, x], axis=1)
    y = b + xp[:, 0:T] * w[0]
    for j in range(1, CONV_W):
        y = y + xp[:, j:j + T] * w[j]
    return y, xp[:, -(CONV_W - 1):]


def rg_lru(x, h0, wa, ba, wx, bx, lam):
    B, T = x.shape[0], x.shape[1]
    xb = x.reshape(B, T, RG_BLOCKS, RG_BLOCK)
    r = jax.nn.sigmoid(jnp.einsum('btgi,gij->btgj', xb, wa).reshape(B, T, RG_WIDTH) + ba)
    i = jax.nn.sigmoid(jnp.einsum('btgi,gij->btgj', xb, wx).reshape(B, T, RG_WIDTH) + bx)
    log_a = -RG_C * r * jax.nn.softplus(-lam)
    a = jnp.exp(log_a)
    u = jnp.sqrt(-jnp.expm1(2.0 * log_a)) * (i * x)
    u = u.at[:, 0].add(a[:, 0] * h0)

    def combine(left, right):
        a1, b1 = left
        a2, b2 = right
        return a1 * a2, a2 * b1 + b2

    _, h = lax.associative_scan(combine, (a, u), axis=1)
    return h, h[:, -1]


def layer(x, S0, h0, buf, g_pre_mix, w_in, w_lr2, b_lr, gla_norm, conv_w, conv_b,
          rg_wa, rg_ba, rg_wx, rg_bx, rg_lambda, w_out, g_post_mix, g_pre_ff, w_ff1, w_ff2,
          g_post_ff):
    B, T = x.shape[0], x.shape[1]
    f32 = jnp.float32
    z = rmsnorm(x, g_pre_mix) @ w_in
    q = z[..., OFF_Q:OFF_K].astype(f32).reshape(B, T, GLA_HEADS, GLA_DK) * (GLA_DK ** -0.5)
    k = z[..., OFF_K:OFF_V].astype(f32).reshape(B, T, GLA_HEADS, GLA_DK)
    v = z[..., OFF_V:OFF_G].astype(f32).reshape(B, T, GLA_HEADS, GLA_DV)
    g = z[..., OFF_G:OFF_LR].astype(f32)
    lr = z[..., OFF_LR:OFF_XR].astype(f32)
    xr = z[..., OFF_XR:OFF_GR].astype(f32)
    gr = z[..., OFF_GR:D_IN].astype(f32)

    log_a = (jax.nn.log_sigmoid(lr @ w_lr2.astype(f32) + b_lr.astype(f32)) / GLA_TAU).reshape(
        B, T, GLA_HEADS, GLA_DK)
    o, S = gla_mix(q, k, v, log_a, S0.astype(f32))
    o = o * lax.rsqrt(jnp.mean(o * o, axis=-1, keepdims=True) + EPS) * gla_norm.astype(f32)
    o = o.reshape(B, T, GLA_WIDTH) * jax.nn.silu(g)

    xc, buf_new = causal_conv(xr, buf.astype(f32), conv_w.astype(f32), conv_b.astype(f32))
    hr, h_last = rg_lru(xc, h0.astype(f32), rg_wa.astype(f32), rg_ba.astype(f32),
                        rg_wx.astype(f32), rg_bx.astype(f32), rg_lambda.astype(f32))
    yr = hr * jax.nn.gelu(gr)

    mix = jnp.concatenate([o, yr], axis=-1).astype(x.dtype) @ w_out
    x = x + rmsnorm(mix, g_post_mix)
    f = jnp.square(jax.nn.relu(rmsnorm(x, g_pre_ff) @ w_ff1)) @ w_ff2
    x = x + rmsnorm(f, g_post_ff)
    return x, S, h_last, buf_new


def setup_inputs(seed: int = 0) -> dict:
    key = jax.random.key(seed)
    ks = jax.random.split(key, 32)
    nrm = lambda i, shape, s: jax.random.normal(ks[i], shape, jnp.float32) * s
    a0 = jax.random.uniform(ks[20], (DEPTH, RG_WIDTH), jnp.float32, 0.9, 0.999)
    s0 = a0 ** (1.0 / RG_C)
    rg_lambda = jnp.log(s0) - jnp.log1p(-s0)
    return {
        'x_prompt': nrm(0, (BATCH, SEQ, D_MODEL), 1.0),
        'x_sample': nrm(1, (DEC_BATCH, DEC_SEQ, D_MODEL), 1.0),
        'state_gla': nrm(2, (DEPTH, DEC_BATCH, GLA_HEADS, GLA_DK, GLA_DV), 0.5),
        'state_rglru': nrm(3, (DEPTH, DEC_BATCH, RG_WIDTH), 0.5),
        'state_conv': nrm(4, (DEPTH, DEC_BATCH, CONV_W - 1, RG_WIDTH), 1.0),
        'g_pre_mix': 1.0 + nrm(5, (DEPTH, D_MODEL), 0.02),
        'w_in': nrm(6, (DEPTH, D_MODEL, D_IN), D_MODEL ** -0.5),
        'w_lr2': nrm(7, (DEPTH, GLA_RANK, GLA_QK), GLA_RANK ** -0.5),
        'b_lr': nrm(8, (DEPTH, GLA_QK), 0.1),
        'gla_norm': 1.0 + nrm(9, (DEPTH, GLA_DV), 0.02),
        'conv_w': nrm(10, (DEPTH, CONV_W, RG_WIDTH), CONV_W ** -0.5),
        'conv_b': nrm(11, (DEPTH, RG_WIDTH), 0.02),
        'rg_wa': nrm(12, (DEPTH, RG_BLOCKS, RG_BLOCK, RG_BLOCK), RG_BLOCK ** -0.5),
        'rg_ba': nrm(13, (DEPTH, RG_WIDTH), 0.1),
        'rg_wx': nrm(14, (DEPTH, RG_BLOCKS, RG_BLOCK, RG_BLOCK), RG_BLOCK ** -0.5),
        'rg_bx': nrm(15, (DEPTH, RG_WIDTH), 0.1),
        'rg_lambda': rg_lambda,
        'w_out': nrm(16, (DEPTH, D_MODEL, D_MODEL), D_MODEL ** -0.5),
        'g_post_mix': 1.0 + nrm(17, (DEPTH, D_MODEL), 0.02),
        'g_pre_ff': 1.0 + nrm(18, (DEPTH, D_MODEL), 0.02),
        'w_ff1': nrm(19, (DEPTH, D_MODEL, D_FF), D_MODEL ** -0.5),
        'w_ff2': nrm(21, (DEPTH, D_FF, D_MODEL), D_FF ** -0.5),
        'g_post_ff': 1.0 + nrm(22, (DEPTH, D_MODEL), 0.02),
    }


def reference(x_prompt, x_sample, state_gla, state_rglru, state_conv, g_pre_mix, w_in, w_lr2,
              b_lr, gla_norm, conv_w, conv_b, rg_wa, rg_ba, rg_wx, rg_bx, rg_lambda, w_out,
              g_post_mix, g_pre_ff, w_ff1, w_ff2, g_post_ff):
    f32 = jnp.float32
    xp = x_prompt
    xs = x_sample
    gla_p, rg_p, cv_p, gla_s, rg_s, cv_s = [], [], [], [], [], []
    for l in range(DEPTH):
        params = (g_pre_mix[l], w_in[l], w_lr2[l], b_lr[l], gla_norm[l], conv_w[l], conv_b[l],
                  rg_wa[l], rg_ba[l], rg_wx[l], rg_bx[l], rg_lambda[l], w_out[l], g_post_mix[l],
                  g_pre_ff[l], w_ff1[l], w_ff2[l], g_post_ff[l])
        xp, S, h, buf = layer(xp,
                              jnp.zeros((BATCH, GLA_HEADS, GLA_DK, GLA_DV), f32),
                              jnp.zeros((BATCH, RG_WIDTH), f32),
                              jnp.zeros((BATCH, CONV_W - 1, RG_WIDTH), f32),
                              *params)
        gla_p.append(S)
        rg_p.append(h)
        cv_p.append(buf)
        xs, S2, h2, buf2 = layer(xs, state_gla[l], state_rglru[l], state_conv[l], *params)
        gla_s.append(S2)
        rg_s.append(h2)
        cv_s.append(buf2)
    gla_prompt = jnp.stack(gla_p)
    rglru_prompt = jnp.stack(rg_p)
    conv_prompt = jnp.stack(cv_p)
    gla_sample = jnp.stack(gla_s)
    rglru_sample = jnp.stack(rg_s)
    conv_sample = jnp.stack(cv_s)
    return (xp, xs, gla_prompt, rglru_prompt, conv_prompt, gla_sample, rglru_sample, conv_sample)
```

```python
import functools

import numpy as np
import jax
import jax.numpy as jnp
from jax import lax
from jax.experimental import pallas as pl
from jax.experimental.pallas import tpu as pltpu

F32 = jnp.float32
BF16 = jnp.bfloat16

D_MODEL = 1024
GLA_HEADS = 4
GLA_DK = 64
GLA_DV = 128
GLA_QK = GLA_HEADS * GLA_DK
GLA_WIDTH = GLA_HEADS * GLA_DV
GLA_RANK = 16
GLA_TAU = 16.0
RG_WIDTH = D_MODEL - GLA_WIDTH
RG_BLOCKS = 8
RG_BLOCK = RG_WIDTH // RG_BLOCKS
RG_C = 8.0
CONV_W = 4
D_FF = 4 * D_MODEL
EPS = 1e-6
CHUNK = 64

LANES = 128
SUBLANES = 8
LR_PAD = LANES
Z_MAIN = 2 * GLA_QK + 2 * GLA_WIDTH + 2 * RG_WIDTH
VMEM_LIMIT = 56 * 1024 * 1024


def _rms(x, g):
    return x * lax.rsqrt(jnp.mean(x * x, axis=-1, keepdims=True) + EPS) * g


def _dot(a, b):
    return jnp.dot(a, b, preferred_element_type=F32)


def _dot_nt(a, b):
    return lax.dot_general(a, b, (((1,), (1,)), ((), ())), preferred_element_type=F32)


def _dot_tn(a, b):
    return lax.dot_general(a, b, (((0,), (0,)), ((), ())), preferred_element_type=F32)


def _split3(x):
    hi = x.astype(BF16)
    r = x - hi.astype(F32)
    mid = r.astype(BF16)
    lo = (r - mid.astype(F32)).astype(BF16)
    return hi, mid, lo


def _in_proj_kernel(x_ref, g_ref, w_ref, wlr1_ref, wlr2_ref, blr_ref,
                    qk_ref, v_ref, gg_ref, la_ref, xr_ref, gr_ref):
    h = _rms(x_ref[...], g_ref[...]).astype(BF16)
    z = _dot(h, w_ref[...])
    o = 0
    q = z[:, o:o + GLA_QK] * (GLA_DK ** -0.5)
    k = z[:, o + GLA_QK:o + 2 * GLA_QK]
    qk_ref[...] = jnp.concatenate([q, k], axis=1)
    o += 2 * GLA_QK
    v_ref[...] = z[:, o:o + GLA_WIDTH]
    o += GLA_WIDTH
    gg_ref[...] = z[:, o:o + GLA_WIDTH]
    o += GLA_WIDTH
    xr_ref[...] = z[:, o:o + RG_WIDTH]
    o += RG_WIDTH
    gr_ref[...] = z[:, o:o + RG_WIDTH]
    lr = _dot(h, wlr1_ref[...])
    pre = _dot(lr.astype(BF16), wlr2_ref[...]) + blr_ref[...]
    la_ref[...] = (jnp.minimum(pre, 0.0) - jnp.log1p(jnp.exp(-jnp.abs(pre)))) * (1.0 / GLA_TAU)


def _in_proj(x2d, g, w_main, w_lr1, w_lr2, b_lr, tm):
    n = x2d.shape[0]
    const = lambda i: (0, 0)
    row = lambda i: (i, 0)
    wspec = lambda shape: pl.BlockSpec(shape, const, pipeline_mode=pl.Buffered(1))
    outs = [(2 * GLA_QK,), (GLA_WIDTH,), (GLA_WIDTH,), (GLA_QK,), (RG_WIDTH,), (RG_WIDTH,)]
    return pl.pallas_call(
        _in_proj_kernel,
        grid=(n // tm,),
        in_specs=[pl.BlockSpec((tm, D_MODEL), row),
                  wspec((1, D_MODEL)),
                  wspec((D_MODEL, Z_MAIN)),
                  wspec((D_MODEL, LR_PAD)),
                  wspec((LR_PAD, GLA_QK)),
                  wspec((1, GLA_QK))],
        out_specs=[pl.BlockSpec((tm, w[0]), row) for w in outs],
        out_shape=[jax.ShapeDtypeStruct((n, w[0]), F32) for w in outs],
        compiler_params=pltpu.CompilerParams(dimension_semantics=("arbitrary",),
                                             vmem_limit_bytes=VMEM_LIMIT),
        name="in_proj",
    )(x2d, g, w_main, w_lr1, w_lr2, b_lr)


def _levels(c):
    m, out = c // 2, []
    while m >= 1:
        out.append(m)
        m //= 2
    return out


@functools.lru_cache(maxsize=None)
def _mixer_consts(c):
    t = np.arange(c)
    le = (t[None, :] <= t[:, None]).astype(np.float32)
    blocks = [le, 1.0 - le]
    masks = []
    for m in _levels(c):
        gpos = (t // (2 * m)) * (2 * m) + m - 1
        blocks.append(le - le[gpos])
        same = (t[:, None] // (2 * m)) == (t[None, :] // (2 * m))
        odd_i = ((t[:, None] // m) % 2) == 1
        even_j = ((t[None, :] // m) % 2) == 0
        masks.append((same & odd_i & even_j).astype(np.float32))
    masks.append(np.eye(c, dtype=np.float32))
    dst = np.concatenate(blocks, axis=0)
    amask = np.stack([np.tile(mk, (GLA_HEADS, 1)) for mk in masks])
    hrow = np.repeat(np.arange(GLA_HEADS), c)
    hcol = np.repeat(np.arange(GLA_HEADS), GLA_DK)
    hmask = (hrow[:, None] == hcol[None, :]).astype(np.float32)
    return dst, amask, hmask


def _gla_chunk(qk, v, la, st, dst, amask, hmask, c):
    nl = len(_levels(c))
    q = qk[:, :GLA_QK]
    k = qk[:, GLA_QK:]
    hi, mid, lo = _split3(la)
    e_all = _dot(dst, hi) + _dot(dst, mid) + _dot(dst, lo)
    b = e_all[0:c]
    brev = e_all[c:2 * c]
    hm = hmask.astype(BF16)

    def stack_heads(x_bf16):
        return jnp.concatenate([x_bf16] * GLA_HEADS, axis=0) * hm

    att = _dot_nt(stack_heads(q.astype(BF16)), k.astype(BF16)) * amask[nl]
    for l in range(nl):
        w = jnp.exp(-jnp.abs(e_all[(2 + l) * c:(3 + l) * c]))
        att = att + _dot_nt(stack_heads((q * w).astype(BF16)), (k * w).astype(BF16)) * amask[l]
    qb = stack_heads((q * jnp.exp(b)).astype(BF16))
    ke = stack_heads((k * jnp.exp(brev)).astype(BF16))
    o = _dot_nt(qb, st.astype(BF16))
    att_bf = att.astype(BF16)
    v_bf = v.astype(BF16)
    o_intra = [_dot(att_bf[h * c:(h + 1) * c], v_bf[:, h * GLA_DV:(h + 1) * GLA_DV])
               for h in range(GLA_HEADS)]
    o = o + jnp.concatenate(o_intra, axis=0)
    v_st = jnp.concatenate([v_bf[:, h * GLA_DV:(h + 1) * GLA_DV] for h in range(GLA_HEADS)], axis=0)
    st_new = st * jnp.exp(b[c - 1:c, :]) + _dot_tn(v_st, ke)
    return o, st_new


def _mixer_kernel(qk_ref, v_ref, gg_ref, la_ref, xr_ref, gr_ref,
                  st0_ref, h0_ref, buf0_ref,
                  gnorm_ref, convw_ref, convb_ref, wrg_ref, ba_ref, bx_ref, lam_ref,
                  dst_ref, amask_ref, hmask_ref,
                  mix_ref, st_out_ref, h_out_ref, buf_out_ref,
                  st_sc, h_sc, xp_sc, *, tb, c):
    t = pl.program_id(1)
    nt = pl.num_programs(1)

    @pl.when(t == 0)
    def _():
        st_sc[...] = st0_ref[...]
        h_sc[...] = jnp.broadcast_to(h0_ref[...], h_sc.shape)
        xp_sc[0:SUBLANES, :] = buf0_ref[...]

    dst = dst_ref[...]
    amask = amask_ref[...]
    hmask = hmask_ref[...]
    gnorm = gnorm_ref[...]

    def chunk_body(ci, st):
        r0 = pl.multiple_of(ci * c, c)
        rows = pl.ds(r0, c)
        o, st_new = _gla_chunk(qk_ref[rows, :], v_ref[rows, :], la_ref[rows, :], st,
                               dst, amask, hmask, c)
        o = o * lax.rsqrt(jnp.mean(o * o, axis=-1, keepdims=True) + EPS) * gnorm
        o = jnp.concatenate([o[h * c:(h + 1) * c] for h in range(GLA_HEADS)], axis=1)
        g = gg_ref[rows, :]
        mix_ref[rows, 0:GLA_WIDTH] = (o * (g * jax.nn.sigmoid(g))).astype(mix_ref.dtype)
        return st_new

    st_fin = lax.fori_loop(0, tb // c, chunk_body, st_sc[...])
    st_sc[...] = st_fin

    xr = xr_ref[...]
    xp_sc[SUBLANES:SUBLANES + tb, :] = xr
    cw = convw_ref[...]
    xc = convb_ref[...] + xp_sc[pl.ds(SUBLANES - 3, tb), :] * cw[0:1, :]
    for j in range(1, CONV_W):
        xc = xc + xp_sc[pl.ds(SUBLANES - 3 + j, tb), :] * cw[j:j + 1, :]
    gates = _dot(xc.astype(BF16), wrg_ref[...])
    r = jax.nn.sigmoid(gates[:, :RG_WIDTH] + ba_ref[...])
    i = jax.nn.sigmoid(gates[:, RG_WIDTH:] + bx_ref[...])
    nlam = -lam_ref[...]
    softplus = jnp.maximum(nlam, 0.0) + jnp.log1p(jnp.exp(-jnp.abs(nlam)))
    log_a = -RG_C * r * softplus
    a = jnp.exp(log_a)
    u = jnp.sqrt((1.0 - a) * (1.0 + a)) * (i * xc)

    sub = lax.broadcasted_iota(jnp.int32, (tb, RG_WIDTH), 0) % SUBLANES
    s = 1
    while s < SUBLANES:
        keep = sub >= s
        u = jnp.where(keep, a * pltpu.roll(u, s, axis=0) + u, u)
        a = jnp.where(keep, a * pltpu.roll(a, s, axis=0), a)
        s *= 2
    carry = h_sc[0:1, :]
    hs = []
    for gi in range(tb // SUBLANES):
        rows = slice(gi * SUBLANES, (gi + 1) * SUBLANES)
        hg = a[rows] * carry + u[rows]
        carry = hg[SUBLANES - 1:SUBLANES, :]
        hs.append(hg)
    hr = jnp.concatenate(hs, axis=0)
    h_sc[...] = jnp.broadcast_to(carry, h_sc.shape)
    xp_sc[0:SUBLANES, :] = xp_sc[tb:tb + SUBLANES, :]
    mix_ref[:, GLA_WIDTH:] = (hr * jax.nn.gelu(gr_ref[...])).astype(mix_ref.dtype)

    @pl.when(t == nt - 1)
    def _():
        st_out_ref[...] = st_fin
        h_out_ref[...] = carry
        buf_out_ref[...] = xp_sc[0:SUBLANES, :]


def _mixer(parts, st0, h0, buf0, prm, tb, c):
    qk = parts[0]
    nb, tt = qk.shape[0], qk.shape[1]
    dst, amask, hmask = _mixer_consts(c)
    seq = lambda w: pl.BlockSpec((None, tb, w), lambda b, t: (b, t, 0))
    per_b = lambda s1, s2: pl.BlockSpec((None, s1, s2), lambda b, t: (b, 0, 0))
    const2 = lambda shape: pl.BlockSpec(shape, lambda b, t: (0,) * len(shape))
    kern = functools.partial(_mixer_kernel, tb=tb, c=c)
    return pl.pallas_call(
        kern,
        grid=(nb, tt // tb),
        in_specs=[seq(2 * GLA_QK), seq(GLA_WIDTH), seq(GLA_WIDTH), seq(GLA_QK), seq(RG_WIDTH), seq(RG_WIDTH),
                  per_b(GLA_DV, GLA_QK), per_b(1, RG_WIDTH), per_b(SUBLANES, RG_WIDTH),
                  const2((1, GLA_DV)), const2((SUBLANES, RG_WIDTH)), const2((1, RG_WIDTH)),
                  const2((RG_WIDTH, 2 * RG_WIDTH)), const2((1, RG_WIDTH)), const2((1, RG_WIDTH)),
                  const2((1, RG_WIDTH)),
                  const2(dst.shape), const2(amask.shape), const2(hmask.shape)],
        out_specs=[seq(D_MODEL), per_b(GLA_DV, GLA_QK), per_b(1, RG_WIDTH), per_b(SUBLANES, RG_WIDTH)],
        out_shape=[jax.ShapeDtypeStruct((nb, tt, D_MODEL), BF16),
                   jax.ShapeDtypeStruct((nb, GLA_DV, GLA_QK), F32),
                   jax.ShapeDtypeStruct((nb, 1, RG_WIDTH), F32),
                   jax.ShapeDtypeStruct((nb, SUBLANES, RG_WIDTH), F32)],
        scratch_shapes=[pltpu.VMEM((GLA_DV, GLA_QK), F32),
                        pltpu.VMEM((SUBLANES, RG_WIDTH), F32),
                        pltpu.VMEM((tb + SUBLANES, RG_WIDTH), F32)],
        compiler_params=pltpu.CompilerParams(dimension_semantics=("arbitrary", "arbitrary"),
                                             vmem_limit_bytes=VMEM_LIMIT),
        name="mixer_c%d" % c,
    )(*parts, st0, h0, buf0, prm["gla_norm"], prm["conv_w"], prm["conv_b"], prm["w_rg"],
      prm["rg_ba"], prm["rg_bx"], prm["rg_lambda"],
      jnp.asarray(dst, BF16), jnp.asarray(amask, F32), jnp.asarray(hmask, F32))


FF_TILE = 1024


def _out_ffn_kernel(x_ref, mix_ref, wout_ref, gpm_ref, gpf_ref, w1_ref, w2_ref, gpo_ref, o_ref):
    mo = _dot(mix_ref[...], wout_ref[...])
    x1 = x_ref[...] + _rms(mo, gpm_ref[...])
    xn = _rms(x1, gpf_ref[...]).astype(BF16)
    f = jnp.zeros(x1.shape, F32)
    for j in range(D_FF // FF_TILE):
        cols = slice(j * FF_TILE, (j + 1) * FF_TILE)
        hdn = jnp.square(jnp.maximum(_dot(xn, w1_ref[:, cols]), 0.0)).astype(BF16)
        f = f + _dot(hdn, w2_ref[cols, :])
    o_ref[...] = x1 + _rms(f, gpo_ref[...])


def _out_ffn(x2d, mix2d, prm, tm):
    n = x2d.shape[0]
    const = lambda i: (0, 0)
    row = lambda i: (i, 0)
    wspec = lambda shape: pl.BlockSpec(shape, const, pipeline_mode=pl.Buffered(1))
    return pl.pallas_call(
        _out_ffn_kernel,
        grid=(n // tm,),
        in_specs=[pl.BlockSpec((tm, D_MODEL), row),
                  pl.BlockSpec((tm, D_MODEL), row),
                  wspec((D_MODEL, D_MODEL)), wspec((1, D_MODEL)), wspec((1, D_MODEL)),
                  wspec((D_MODEL, D_FF)), wspec((D_FF, D_MODEL)), wspec((1, D_MODEL))],
        out_specs=pl.BlockSpec((tm, D_MODEL), row),
        out_shape=jax.ShapeDtypeStruct((n, D_MODEL), F32),
        compiler_params=pltpu.CompilerParams(dimension_semantics=("arbitrary",),
                                             vmem_limit_bytes=VMEM_LIMIT),
        name="out_ffn",
    )(x2d, mix2d, prm["w_out"], prm["g_post_mix"], prm["g_pre_ff"], prm["w_ff1"], prm["w_ff2"],
      prm["g_post_ff"])


def _prep_layer(l, g_pre_mix, w_in, w_lr2, b_lr, gla_norm, conv_w, conv_b, rg_wa, rg_ba, rg_wx,
                rg_bx, rg_lambda, w_out, g_post_mix, g_pre_ff, w_ff1, w_ff2, g_post_ff):
    off_lr = 2 * GLA_QK + 2 * GLA_WIDTH
    off_xr = off_lr + GLA_RANK
    w = w_in[l]
    w_main = jnp.concatenate([w[:, :off_lr], w[:, off_xr:]], axis=1).astype(BF16)
    w_lr1 = jnp.pad(w[:, off_lr:off_xr], ((0, 0), (0, LR_PAD - GLA_RANK))).astype(BF16)
    w_lr2p = jnp.pad(w_lr2[l], ((0, LR_PAD - GLA_RANK), (0, 0))).astype(BF16)
    eye = jnp.eye(RG_BLOCKS, dtype=F32)
    bd = lambda wb: jnp.einsum("gij,gh->gihj", wb, eye).reshape(RG_WIDTH, RG_WIDTH)
    w_rg = jnp.concatenate([bd(rg_wa[l]), bd(rg_wx[l])], axis=1).astype(BF16)
    row = lambda a: a[l].reshape(1, -1)
    return dict(
        g_pre_mix=row(g_pre_mix), w_main=w_main, w_lr1=w_lr1, w_lr2=w_lr2p, b_lr=row(b_lr),
        gla_norm=row(gla_norm),
        conv_w=jnp.pad(conv_w[l], ((0, SUBLANES - CONV_W), (0, 0))), conv_b=row(conv_b),
        w_rg=w_rg, rg_ba=row(rg_ba), rg_bx=row(rg_bx), rg_lambda=row(rg_lambda),
        w_out=w_out[l].astype(BF16), g_post_mix=row(g_post_mix), g_pre_ff=row(g_pre_ff),
        w_ff1=w_ff1[l].astype(BF16), w_ff2=w_ff2[l].astype(BF16), g_post_ff=row(g_post_ff))


def _layer(x, st0, h0, buf0, prm, tb, c, tm):
    nb, tt, _ = x.shape
    x2d = x.reshape(nb * tt, D_MODEL)
    parts = _in_proj(x2d, prm["g_pre_mix"], prm["w_main"], prm["w_lr1"], prm["w_lr2"], prm["b_lr"], tm)
    parts = [p.reshape(nb, tt, p.shape[-1]) for p in parts]
    mix, st, h, buf = _mixer(parts, st0, h0, buf0, prm, tb, c)
    x_new = _out_ffn(x2d, mix.reshape(nb * tt, D_MODEL), prm, tm)
    return x_new.reshape(nb, tt, D_MODEL), st, h, buf


def _state_in(s):
    return jnp.transpose(s, (0, 3, 1, 2)).reshape(s.shape[0], GLA_DV, GLA_QK)


def _state_out(st):
    return jnp.transpose(st.reshape(st.shape[0], GLA_DV, GLA_HEADS, GLA_DK), (0, 2, 3, 1))


def kernel(x_prompt, x_sample, state_gla, state_rglru, state_conv, g_pre_mix, w_in, w_lr2, b_lr, gla_norm, conv_w, conv_b, rg_wa, rg_ba, rg_wx, rg_bx, rg_lambda, w_out, g_post_mix, g_pre_ff, w_ff1, w_ff2, g_post_ff):
    depth = w_in.shape[0]
    bp, tp, _ = x_prompt.shape
    bs, ts, _ = x_sample.shape
    cp, cs = min(CHUNK, tp), min(CHUNK, ts)
    tbp, tbs = min(256, tp), min(256, ts)
    pad_buf = ((0, 0), (SUBLANES - (CONV_W - 1), 0), (0, 0))
    xp, xs = x_prompt, x_sample
    outs = [[] for _ in range(6)]
    for l in range(depth):
        prm = _prep_layer(l, g_pre_mix, w_in, w_lr2, b_lr, gla_norm, conv_w, conv_b, rg_wa, rg_ba,
                          rg_wx, rg_bx, rg_lambda, w_out, g_post_mix, g_pre_ff, w_ff1, w_ff2, g_post_ff)
        xp, st, h, buf = _layer(xp,
                                jnp.zeros((bp, GLA_DV, GLA_QK), F32),
                                jnp.zeros((bp, 1, RG_WIDTH), F32),
                                jnp.zeros((bp, SUBLANES, RG_WIDTH), F32),
                                prm, tbp, cp, 512)
        outs[0].append(_state_out(st))
        outs[1].append(h[:, 0, :])
        outs[2].append(buf[:, SUBLANES - (CONV_W - 1):, :])
        xs, st, h, buf = _layer(xs,
                                _state_in(state_gla[l]),
                                state_rglru[l][:, None, :],
                                jnp.pad(state_conv[l], pad_buf),
                                prm, tbs, cs, 512)
        outs[3].append(_state_out(st))
        outs[4].append(h[:, 0, :])
        outs[5].append(buf[:, SUBLANES - (CONV_W - 1):, :])
    return (xp, xs) + tuple(jnp.stack(o) for o in outs)
```

```python
import functools

import numpy as np
import jax
import jax.numpy as jnp
from jax import lax
from jax.experimental import pallas as pl
from jax.experimental.pallas import tpu as pltpu

F32 = jnp.float32
BF16 = jnp.bfloat16

D_MODEL = 1024
GLA_HEADS = 4
GLA_DK = 64
GLA_DV = 128
GLA_QK = GLA_HEADS * GLA_DK
GLA_WIDTH = GLA_HEADS * GLA_DV
GLA_RANK = 16
GLA_TAU = 16.0
RG_WIDTH = D_MODEL - GLA_WIDTH
RG_BLOCKS = 8
RG_BLOCK = RG_WIDTH // RG_BLOCKS
RG_C = 8.0
CONV_W = 4
D_FF = 4 * D_MODEL
EPS = 1e-6
CHUNK = 64
LOG2E = 1.4426950408889634

LANES = 128
SUBLANES = 8
LR_PAD = LANES
Z_MAIN = 2 * GLA_QK + 2 * GLA_WIDTH + 2 * RG_WIDTH
VMEM_LIMIT = 56 * 1024 * 1024


def _rms(x, g):
    return x * lax.rsqrt(jnp.mean(x * x, axis=-1, keepdims=True) + EPS) * g


def _dot(a, b):
    return jnp.dot(a, b, preferred_element_type=F32)


def _dot_nt(a, b):
    return lax.dot_general(a, b, (((1,), (1,)), ((), ())), preferred_element_type=F32)


def _dot_tn(a, b):
    return lax.dot_general(a, b, (((0,), (0,)), ((), ())), preferred_element_type=F32)


def _split3(x):
    hi = x.astype(BF16)
    r = x - hi.astype(F32)
    mid = r.astype(BF16)
    lo = (r - mid.astype(F32)).astype(BF16)
    return hi, mid, lo


def _in_proj_kernel(x_ref, g_ref, w_ref, wlr1_ref, wlr2_ref, blr_ref,
                    qk_ref, v_ref, gg_ref, la_ref, xr_ref, gr_ref):
    h = _rms(x_ref[...], g_ref[...]).astype(BF16)
    z = _dot(h, w_ref[...])
    o = 0
    q = z[:, o:o + GLA_QK] * (GLA_DK ** -0.5)
    k = z[:, o + GLA_QK:o + 2 * GLA_QK]
    qk_ref[...] = jnp.concatenate([q, k], axis=1)
    o += 2 * GLA_QK
    v_ref[...] = z[:, o:o + GLA_WIDTH]
    o += GLA_WIDTH
    gg_ref[...] = z[:, o:o + GLA_WIDTH]
    o += GLA_WIDTH
    xr_ref[...] = z[:, o:o + RG_WIDTH]
    o += RG_WIDTH
    gr_ref[...] = z[:, o:o + RG_WIDTH]
    lr = _dot(h, wlr1_ref[...])
    pre = _dot(lr.astype(BF16), wlr2_ref[...]) + blr_ref[...]
    la_ref[...] = (jnp.minimum(pre, 0.0) - jnp.log1p(jnp.exp(-jnp.abs(pre)))) * (LOG2E / GLA_TAU)


def _in_proj(x2d, g, w_main, w_lr1, w_lr2, b_lr, tm):
    n = x2d.shape[0]
    const = lambda i: (0, 0)
    row = lambda i: (i, 0)
    wspec = lambda shape: pl.BlockSpec(shape, const, pipeline_mode=pl.Buffered(1))
    outs = [(2 * GLA_QK,), (GLA_WIDTH,), (GLA_WIDTH,), (GLA_QK,), (RG_WIDTH,), (RG_WIDTH,)]
    return pl.pallas_call(
        _in_proj_kernel,
        grid=(n // tm,),
        in_specs=[pl.BlockSpec((tm, D_MODEL), row),
                  wspec((1, D_MODEL)),
                  wspec((D_MODEL, Z_MAIN)),
                  wspec((D_MODEL, LR_PAD)),
                  wspec((LR_PAD, GLA_QK)),
                  wspec((1, GLA_QK))],
        out_specs=[pl.BlockSpec((tm, w[0]), row) for w in outs],
        out_shape=[jax.ShapeDtypeStruct((n, w[0]), F32) for w in outs],
        compiler_params=pltpu.CompilerParams(dimension_semantics=("arbitrary",),
                                             vmem_limit_bytes=VMEM_LIMIT),
        name="in_proj",
    )(x2d, g, w_main, w_lr1, w_lr2, b_lr)


def _levels(c):
    m, out = c // 2, []
    while m >= 1:
        out.append(m)
        m //= 2
    return out


@functools.lru_cache(maxsize=None)
def _mixer_consts(c):
    t = np.arange(c)
    tri = (t[None, :] <= t[:, None]).astype(np.float32)
    masks = []
    for m in _levels(c):
        same = (t[:, None] // (2 * m)) == (t[None, :] // (2 * m))
        odd_i = ((t[:, None] // m) % 2) == 1
        even_j = ((t[None, :] // m) % 2) == 0
        masks.append((same & odd_i & even_j).astype(np.float32))
    masks.append(np.eye(c, dtype=np.float32))
    amask = np.stack([np.tile(mk, (1, GLA_HEADS)) for mk in masks])
    hrow = np.repeat(np.arange(GLA_HEADS), c)
    hcol = np.repeat(np.arange(GLA_HEADS), GLA_DK)
    kmask = (hrow[:, None] == hcol[None, :]).astype(np.float32)
    return tri, amask, kmask


def _tile3(x):
    return x.reshape(x.shape[0] // SUBLANES, SUBLANES, x.shape[1])


def _neg_abs(x):
    bits = lax.bitcast_convert_type(x, jnp.uint32) | jnp.uint32(0x80000000)
    return lax.bitcast_convert_type(bits, F32)


def _level_neg_exponents(b, c, nc):
    g = c // SUBLANES
    b3 = _tile3(b)
    sub = lax.broadcasted_iota(jnp.int32, b3.shape, 1)
    ends = {1: b3}
    cur = b3
    for m in (2, 4, 8):
        half = m // 2
        cur = jnp.where((sub & half) != 0, cur, pltpu.roll(cur, SUBLANES - half, axis=1))
        ends[m] = cur
    tile_end = [cur[j] for j in range(nc * g)]
    out = {}
    for m in _levels(c):
        if m < SUBLANES:
            e = ends[m]
            ref = jnp.where((sub & m) != 0, pltpu.roll(e, m, axis=1), e)
        else:
            tm = m // SUBLANES
            ref = jnp.stack([tile_end[(j // g) * g + ((j % g) // (2 * tm)) * (2 * tm) + tm - 1]
                             for j in range(nc * g)])
        out[m] = _neg_abs(b3 - ref).reshape(b.shape)
    return out, [tile_end[ci * g + g - 1] for ci in range(nc)]


def _gla_block(q, k, v, la2, st, tri, amask, kmask, c, nc):
    levels = _levels(c)
    g = c // SUBLANES
    rows = lambda x, ci: x[ci * c:(ci + 1) * c]
    hi, mid, lo = _split3(la2)
    b = _dot(tri, hi) + _dot(tri, mid) + _dot(tri, lo)
    negs, b_last = _level_neg_exponents(b, c, nc)
    km = kmask.astype(BF16)

    def stack_heads(x_bf16):
        return jnp.concatenate([x_bf16] * GLA_HEADS, axis=0) * km

    q_bf, k_bf = q.astype(BF16), k.astype(BF16)
    att = [_dot_nt(rows(q_bf, ci), stack_heads(rows(k_bf, ci))) * amask[len(levels)] for ci in range(nc)]
    for l, m in enumerate(levels):
        w = jnp.exp2(negs[m])
        qw, kw = (q * w).astype(BF16), (k * w).astype(BF16)
        for ci in range(nc):
            att[ci] = att[ci] + _dot_nt(rows(qw, ci), stack_heads(rows(kw, ci))) * amask[l]
    b3 = _tile3(b)
    brev = (jnp.stack([b_last[j // g] for j in range(nc * g)]) - b3).reshape(b.shape)
    ke = (k * jnp.exp2(brev)).astype(BF16)
    qb = (q * jnp.exp2(b)).astype(BF16)
    v_bf = v.astype(BF16)
    v_heads = [[rows(v_bf, ci)[:, h * GLA_DV:(h + 1) * GLA_DV] for h in range(GLA_HEADS)] for ci in range(nc)]
    upd = [_dot_tn(jnp.concatenate(v_heads[ci], axis=0), stack_heads(rows(ke, ci))) for ci in range(nc)]
    sts = [st]
    for ci in range(nc):
        sts.append(sts[ci] * jnp.exp2(b_last[ci][0:1, :]) + upd[ci])
    outs = []
    for ci in range(nc):
        att_bf = att[ci].astype(BF16)
        o_intra = [_dot(att_bf[:, h * c:(h + 1) * c], v_heads[ci][h]) for h in range(GLA_HEADS)]
        o_inter = _dot_nt(stack_heads(rows(qb, ci)), sts[ci].astype(BF16))
        outs.append(o_inter + jnp.concatenate(o_intra, axis=0))
    return outs, sts[nc]


def _mixer_kernel(qk_ref, v_ref, gg_ref, la_ref, xr_ref, gr_ref,
                  st0_ref, h0_ref, buf0_ref,
                  gnorm_ref, convw_ref, convb_ref, wrg_ref, ba_ref, bx_ref, lam_ref,
                  tri_ref, amask_ref, kmask_ref,
                  mix_ref, st_out_ref, h_out_ref, buf_out_ref,
                  st_sc, h_sc, tail_sc, *, tb, c):
    t = pl.program_id(1)
    nt = pl.num_programs(1)
    nc = tb // c

    @pl.when(t == 0)
    def _():
        st_sc[...] = st0_ref[...]
        h_sc[...] = jnp.broadcast_to(h0_ref[...], h_sc.shape)
        tail_sc[...] = buf0_ref[...]

    outs, st_fin = _gla_block(qk_ref[:, 0:GLA_QK], qk_ref[:, GLA_QK:2 * GLA_QK], v_ref[...], la_ref[...],
                              st_sc[...], tri_ref[...], amask_ref[...], kmask_ref[...], c, nc)
    st_sc[...] = st_fin
    o = jnp.concatenate(outs, axis=0)
    o = o * lax.rsqrt(jnp.mean(o * o, axis=-1, keepdims=True) + EPS) * gnorm_ref[...]
    o = jnp.concatenate([jnp.concatenate([o[(ci * GLA_HEADS + h) * c:(ci * GLA_HEADS + h + 1) * c]
                                          for h in range(GLA_HEADS)], axis=1) for ci in range(nc)], axis=0)
    g = gg_ref[...]
    mix_ref[:, 0:GLA_WIDTH] = (o * (g * jax.nn.sigmoid(g))).astype(mix_ref.dtype)

    ng = tb // SUBLANES
    x3 = _tile3(xr_ref[...])
    xe = jnp.concatenate([tail_sc[...][None], x3], axis=0)
    sub = lax.broadcasted_iota(jnp.int32, x3.shape, 1)
    cw = convw_ref[...]
    xc = convb_ref[...] + x3 * cw[CONV_W - 1:CONV_W, :]
    for s in range(1, CONV_W):
        rolled = pltpu.roll(xe, s, axis=1)
        shifted = jnp.where(sub >= s, rolled[1:], rolled[:-1])
        xc = xc + shifted * cw[CONV_W - 1 - s:CONV_W - s, :]
    gates = _tile3(_dot(xc.reshape(tb, RG_WIDTH).astype(BF16), wrg_ref[...]))
    r = jax.nn.sigmoid(gates[:, :, :RG_WIDTH] + ba_ref[...])
    i = jax.nn.sigmoid(gates[:, :, RG_WIDTH:] + bx_ref[...])
    nlam = -lam_ref[...]
    softplus = jnp.maximum(nlam, 0.0) + jnp.log1p(jnp.exp(-jnp.abs(nlam)))
    log_a = -RG_C * r * softplus
    a = jnp.exp(log_a)
    u = jnp.sqrt((1.0 - a) * (1.0 + a)) * (i * xc)

    s = 1
    while s < SUBLANES:
        keep = sub >= s
        u = jnp.where(keep, a * pltpu.roll(u, s, axis=1) + u, u)
        a = jnp.where(keep, a * pltpu.roll(a, s, axis=1), a)
        s *= 2
    carry = h_sc[0:1, :]
    hs = []
    for gi in range(ng):
        hg = a[gi] * carry + u[gi]
        carry = hg[SUBLANES - 1:SUBLANES, :]
        hs.append(hg)
    hr = jnp.stack(hs).reshape(tb, RG_WIDTH)
    h_sc[...] = jnp.broadcast_to(carry, h_sc.shape)
    tail_sc[...] = x3[ng - 1]
    mix_ref[:, GLA_WIDTH:] = (hr * jax.nn.gelu(gr_ref[...])).astype(mix_ref.dtype)

    @pl.when(t == nt - 1)
    def _():
        st_out_ref[...] = st_fin
        h_out_ref[...] = carry
        buf_out_ref[...] = x3[ng - 1]


def _mixer(parts, st0, h0, buf0, prm, tb, c):
    qk = parts[0]
    nb, tt = qk.shape[0], qk.shape[1]
    tri, amask, kmask = _mixer_consts(c)
    tri = np.kron(np.eye(tb // c, dtype=np.float32), tri)
    seq = lambda w: pl.BlockSpec((None, tb, w), lambda b, t: (b, t, 0))
    per_b = lambda s1, s2: pl.BlockSpec((None, s1, s2), lambda b, t: (b, 0, 0))
    const2 = lambda shape: pl.BlockSpec(shape, lambda b, t: (0,) * len(shape))
    kern = functools.partial(_mixer_kernel, tb=tb, c=c)
    return pl.pallas_call(
        kern,
        grid=(nb, tt // tb),
        in_specs=[seq(2 * GLA_QK), seq(GLA_WIDTH), seq(GLA_WIDTH), seq(GLA_QK), seq(RG_WIDTH), seq(RG_WIDTH),
                  per_b(GLA_DV, GLA_QK), per_b(1, RG_WIDTH), per_b(SUBLANES, RG_WIDTH),
                  const2((1, GLA_DV)), const2((SUBLANES, RG_WIDTH)), const2((1, RG_WIDTH)),
                  const2((RG_WIDTH, 2 * RG_WIDTH)), const2((1, RG_WIDTH)), const2((1, RG_WIDTH)),
                  const2((1, RG_WIDTH)),
                  const2(tri.shape), const2(amask.shape), const2(kmask.shape)],
        out_specs=[seq(D_MODEL), per_b(GLA_DV, GLA_QK), per_b(1, RG_WIDTH), per_b(SUBLANES, RG_WIDTH)],
        out_shape=[jax.ShapeDtypeStruct((nb, tt, D_MODEL), BF16),
                   jax.ShapeDtypeStruct((nb, GLA_DV, GLA_QK), F32),
                   jax.ShapeDtypeStruct((nb, 1, RG_WIDTH), F32),
                   jax.ShapeDtypeStruct((nb, SUBLANES, RG_WIDTH), F32)],
        scratch_shapes=[pltpu.VMEM((GLA_DV, GLA_QK), F32),
                        pltpu.VMEM((SUBLANES, RG_WIDTH), F32),
                        pltpu.VMEM((SUBLANES, RG_WIDTH), F32)],
        compiler_params=pltpu.CompilerParams(dimension_semantics=("arbitrary", "arbitrary"),
                                             vmem_limit_bytes=VMEM_LIMIT),
        name="mixer_c%d" % c,
    )(*parts, st0, h0, buf0, prm["gla_norm"], prm["conv_w"], prm["conv_b"], prm["w_rg"],
      prm["rg_ba"], prm["rg_bx"], prm["rg_lambda"],
      jnp.asarray(tri, BF16), jnp.asarray(amask, F32), jnp.asarray(kmask, F32))


FF_TILE = 1024


def _out_ffn_kernel(x_ref, mix_ref, wout_ref, gpm_ref, gpf_ref, w1_ref, w2_ref, gpo_ref, o_ref):
    mo = _dot(mix_ref[...], wout_ref[...])
    x1 = x_ref[...] + _rms(mo, gpm_ref[...])
    xn = _rms(x1, gpf_ref[...]).astype(BF16)
    f = jnp.zeros(x1.shape, F32)
    for j in range(D_FF // FF_TILE):
        cols = slice(j * FF_TILE, (j + 1) * FF_TILE)
        hdn = jnp.square(jnp.maximum(_dot(xn, w1_ref[:, cols]), 0.0)).astype(BF16)
        f = f + _dot(hdn, w2_ref[cols, :])
    o_ref[...] = x1 + _rms(f, gpo_ref[...])


def _out_ffn(x2d, mix2d, prm, tm):
    n = x2d.shape[0]
    const = lambda i: (0, 0)
    row = lambda i: (i, 0)
    wspec = lambda shape: pl.BlockSpec(shape, const, pipeline_mode=pl.Buffered(1))
    return pl.pallas_call(
        _out_ffn_kernel,
        grid=(n // tm,),
        in_specs=[pl.BlockSpec((tm, D_MODEL), row),
                  pl.BlockSpec((tm, D_MODEL), row),
                  wspec((D_MODEL, D_MODEL)), wspec((1, D_MODEL)), wspec((1, D_MODEL)),
                  wspec((D_MODEL, D_FF)), wspec((D_FF, D_MODEL)), wspec((1, D_MODEL))],
        out_specs=pl.BlockSpec((tm, D_MODEL), row),
        out_shape=jax.ShapeDtypeStruct((n, D_MODEL), F32),
        compiler_params=pltpu.CompilerParams(dimension_semantics=("arbitrary",),
                                             vmem_limit_bytes=VMEM_LIMIT),
        name="out_ffn",
    )(x2d, mix2d, prm["w_out"], prm["g_post_mix"], prm["g_pre_ff"], prm["w_ff1"], prm["w_ff2"],
      prm["g_post_ff"])


def _prep_layer(l, g_pre_mix, w_in, w_lr2, b_lr, gla_norm, conv_w, conv_b, rg_wa, rg_ba, rg_wx,
                rg_bx, rg_lambda, w_out, g_post_mix, g_pre_ff, w_ff1, w_ff2, g_post_ff):
    off_lr = 2 * GLA_QK + 2 * GLA_WIDTH
    off_xr = off_lr + GLA_RANK
    w = w_in[l]
    w_main = jnp.concatenate([w[:, :off_lr], w[:, off_xr:]], axis=1).astype(BF16)
    w_lr1 = jnp.pad(w[:, off_lr:off_xr], ((0, 0), (0, LR_PAD - GLA_RANK))).astype(BF16)
    w_lr2p = jnp.pad(w_lr2[l], ((0, LR_PAD - GLA_RANK), (0, 0))).astype(BF16)
    eye = jnp.eye(RG_BLOCKS, dtype=F32)
    bd = lambda wb: jnp.einsum("gij,gh->gihj", wb, eye).reshape(RG_WIDTH, RG_WIDTH)
    w_rg = jnp.concatenate([bd(rg_wa[l]), bd(rg_wx[l])], axis=1).astype(BF16)
    row = lambda a: a[l].reshape(1, -1)
    return dict(
        g_pre_mix=row(g_pre_mix), w_main=w_main, w_lr1=w_lr1, w_lr2=w_lr2p, b_lr=row(b_lr),
        gla_norm=row(gla_norm),
        conv_w=jnp.pad(conv_w[l], ((0, SUBLANES - CONV_W), (0, 0))), conv_b=row(conv_b),
        w_rg=w_rg, rg_ba=row(rg_ba), rg_bx=row(rg_bx), rg_lambda=row(rg_lambda),
        w_out=w_out[l].astype(BF16), g_post_mix=row(g_post_mix), g_pre_ff=row(g_pre_ff),
        w_ff1=w_ff1[l].astype(BF16), w_ff2=w_ff2[l].astype(BF16), g_post_ff=row(g_post_ff))


def _layer(x, st0, h0, buf0, prm, tb, c, tm):
    nb, tt, _ = x.shape
    x2d = x.reshape(nb * tt, D_MODEL)
    parts = _in_proj(x2d, prm["g_pre_mix"], prm["w_main"], prm["w_lr1"], prm["w_lr2"], prm["b_lr"], tm)
    parts = [p.reshape(nb, tt, p.shape[-1]) for p in parts]
    mix, st, h, buf = _mixer(parts, st0, h0, buf0, prm, tb, c)
    x_new = _out_ffn(x2d, mix.reshape(nb * tt, D_MODEL), prm, tm)
    return x_new.reshape(nb, tt, D_MODEL), st, h, buf


def _state_in(s):
    return jnp.transpose(s, (0, 3, 1, 2)).reshape(s.shape[0], GLA_DV, GLA_QK)


def _state_out(st):
    return jnp.transpose(st.reshape(st.shape[0], GLA_DV, GLA_HEADS, GLA_DK), (0, 2, 3, 1))


def kernel(x_prompt, x_sample, state_gla, state_rglru, state_conv, g_pre_mix, w_in, w_lr2, b_lr, gla_norm, conv_w, conv_b, rg_wa, rg_ba, rg_wx, rg_bx, rg_lambda, w_out, g_post_mix, g_pre_ff, w_ff1, w_ff2, g_post_ff):
    depth = w_in.shape[0]
    bp, tp, _ = x_prompt.shape
    bs, ts, _ = x_sample.shape
    cp, cs = min(CHUNK, tp), min(CHUNK, ts)
    tbp, tbs = min(256, tp), min(256, ts)
    pad_buf = ((0, 0), (SUBLANES - (CONV_W - 1), 0), (0, 0))
    xp, xs = x_prompt, x_sample
    outs = [[] for _ in range(6)]
    for l in range(depth):
        prm = _prep_layer(l, g_pre_mix, w_in, w_lr2, b_lr, gla_norm, conv_w, conv_b, rg_wa, rg_ba,
                          rg_wx, rg_bx, rg_lambda, w_out, g_post_mix, g_pre_ff, w_ff1, w_ff2, g_post_ff)
        xp, st, h, buf = _layer(xp,
                                jnp.zeros((bp, GLA_DV, GLA_QK), F32),
                                jnp.zeros((bp, 1, RG_WIDTH), F32),
                                jnp.zeros((bp, SUBLANES, RG_WIDTH), F32),
                                prm, tbp, cp, 512)
        outs[0].append(_state_out(st))
        outs[1].append(h[:, 0, :])
        outs[2].append(buf[:, SUBLANES - (CONV_W - 1):, :])
        xs, st, h, buf = _layer(xs,
                                _state_in(state_gla[l]),
                                state_rglru[l][:, None, :],
                                jnp.pad(state_conv[l], pad_buf),
                                prm, tbs, cs, 512)
        outs[3].append(_state_out(st))
        outs[4].append(h[:, 0, :])
        outs[5].append(buf[:, SUBLANES - (CONV_W - 1):, :])
    return (xp, xs) + tuple(jnp.stack(o) for o in outs)
```

```python
import functools

import numpy as np
import jax
import jax.numpy as jnp
from jax import lax
from jax.experimental import pallas as pl
from jax.experimental.pallas import tpu as pltpu

F32 = jnp.float32
BF16 = jnp.bfloat16

D_MODEL = 1024
GLA_HEADS = 4
GLA_DK = 64
GLA_DV = 128
GLA_QK = GLA_HEADS * GLA_DK
GLA_WIDTH = GLA_HEADS * GLA_DV
GLA_RANK = 16
GLA_TAU = 16.0
RG_WIDTH = D_MODEL - GLA_WIDTH
RG_BLOCKS = 8
RG_BLOCK = RG_WIDTH // RG_BLOCKS
RG_C = 8.0
CONV_W = 4
D_FF = 4 * D_MODEL
EPS = 1e-6
CHUNK = 64
LOG2E = 1.4426950408889634

LANES = 128
SUBLANES = 8
LR_PAD = LANES
Z_MAIN = 2 * GLA_QK + 2 * GLA_WIDTH + 2 * RG_WIDTH
VMEM_LIMIT = 56 * 1024 * 1024


def _rms(x, g):
    return x * lax.rsqrt(jnp.mean(x * x, axis=-1, keepdims=True) + EPS) * g


def _dot(a, b):
    return jnp.dot(a, b, preferred_element_type=F32)


def _dot_nt(a, b):
    return lax.dot_general(a, b, (((1,), (1,)), ((), ())), preferred_element_type=F32)


def _dot_tn(a, b):
    return lax.dot_general(a, b, (((0,), (0,)), ((), ())), preferred_element_type=F32)


def _split3(x):
    hi = x.astype(BF16)
    r = x - hi.astype(F32)
    mid = r.astype(BF16)
    lo = (r - mid.astype(F32)).astype(BF16)
    return hi, mid, lo


def _in_proj_kernel(x_ref, g_ref, w_ref, wlr1_ref, wlr2_ref, blr_ref,
                    qk_ref, v_ref, gg_ref, la_ref, xr_ref, gr_ref):
    h = _rms(x_ref[...], g_ref[...]).astype(BF16)
    z = _dot(h, w_ref[...])
    o = 0
    q = z[:, o:o + GLA_QK] * (GLA_DK ** -0.5)
    k = z[:, o + GLA_QK:o + 2 * GLA_QK]
    qk_ref[...] = jnp.concatenate([q, k], axis=1)
    o += 2 * GLA_QK
    v_ref[...] = z[:, o:o + GLA_WIDTH]
    o += GLA_WIDTH
    gg_ref[...] = z[:, o:o + GLA_WIDTH]
    o += GLA_WIDTH
    xr_ref[...] = z[:, o:o + RG_WIDTH]
    o += RG_WIDTH
    gr_ref[...] = z[:, o:o + RG_WIDTH]
    lr = _dot(h, wlr1_ref[...])
    pre = _dot(lr.astype(BF16), wlr2_ref[...]) + blr_ref[...]
    la_ref[...] = (jnp.minimum(pre, 0.0) - jnp.log1p(jnp.exp(-jnp.abs(pre)))) * (LOG2E / GLA_TAU)


def _in_proj(x2d, g, w_main, w_lr1, w_lr2, b_lr, tm):
    n = x2d.shape[0]
    const = lambda i: (0, 0)
    row = lambda i: (i, 0)
    wspec = lambda shape: pl.BlockSpec(shape, const, pipeline_mode=pl.Buffered(1))
    outs = [(2 * GLA_QK,), (GLA_WIDTH,), (GLA_WIDTH,), (GLA_QK,), (RG_WIDTH,), (RG_WIDTH,)]
    return pl.pallas_call(
        _in_proj_kernel,
        grid=(n // tm,),
        in_specs=[pl.BlockSpec((tm, D_MODEL), row),
                  wspec((1, D_MODEL)),
                  wspec((D_MODEL, Z_MAIN)),
                  wspec((D_MODEL, LR_PAD)),
                  wspec((LR_PAD, GLA_QK)),
                  wspec((1, GLA_QK))],
        out_specs=[pl.BlockSpec((tm, w[0]), row) for w in outs],
        out_shape=[jax.ShapeDtypeStruct((n, w[0]), F32) for w in outs],
        compiler_params=pltpu.CompilerParams(dimension_semantics=("arbitrary",),
                                             vmem_limit_bytes=VMEM_LIMIT),
        name="in_proj",
    )(x2d, g, w_main, w_lr1, w_lr2, b_lr)


def _levels(c):
    m, out = c // 2, []
    while m >= 1:
        out.append(m)
        m //= 2
    return out


@functools.lru_cache(maxsize=None)
def _mixer_consts(c):
    t = np.arange(c)
    tri = (t[None, :] <= t[:, None]).astype(np.float32)
    masks = []
    for m in _levels(c):
        same = (t[:, None] // (2 * m)) == (t[None, :] // (2 * m))
        odd_i = ((t[:, None] // m) % 2) == 1
        even_j = ((t[None, :] // m) % 2) == 0
        masks.append((same & odd_i & even_j).astype(np.float32))
    masks.append(np.eye(c, dtype=np.float32))
    amask = np.stack([np.tile(mk, (1, GLA_HEADS)) for mk in masks])
    hrow = np.repeat(np.arange(GLA_HEADS), c)
    hcol = np.repeat(np.arange(GLA_HEADS), GLA_DK)
    kmask = (hrow[:, None] == hcol[None, :]).astype(np.float32)
    return tri, amask, kmask


def _tile3(x):
    return x.reshape(x.shape[0] // SUBLANES, SUBLANES, x.shape[1])


def _level_neg_exponents(b, c, nch):
    g = c // SUBLANES
    b3 = _tile3(b)
    sub = lax.broadcasted_iota(jnp.int32, b3.shape, 1)
    sub1 = lax.broadcasted_iota(jnp.int32, (1,) + b3.shape[1:], 1)
    ends = {1: b3}
    cur = b3
    for m in (2, 4, 8):
        half = m // 2
        cur = jnp.where((sub & half) != 0, cur, pltpu.roll(cur, SUBLANES - half, axis=1))
        ends[m] = cur
    tile_end = [cur[j] for j in range(nch * g)]
    out = {}
    for m in _levels(c):
        if m < SUBLANES:
            e = ends[m]
            ref = jnp.where((sub & m) != 0, pltpu.roll(e, m, axis=1), e)
            sign = jnp.where((sub1 & m) != 0, 1.0, -1.0)
            neg = (b3 - ref) * sign
        else:
            tm = m // SUBLANES
            tiles = []
            for j in range(nch * g):
                jl = j % g
                ref = tile_end[j - jl + (jl // (2 * tm)) * (2 * tm) + tm - 1]
                tiles.append(b3[j] - ref if (jl // tm) % 2 == 1 else ref - b3[j])
            neg = jnp.stack(tiles)
        out[m] = neg.reshape(b.shape)
    return out, [tile_end[ci * g + g - 1] for ci in range(nch)]


def _gla_block(q, k, v, la2, sts, tri, amask, kmask, c, ns, nc):
    levels = _levels(c)
    nch = ns * nc
    g = c // SUBLANES
    rows = lambda x, ci: x[ci * c:(ci + 1) * c]
    hi, mid, lo = _split3(la2)
    b = _dot(tri, hi) + _dot(tri, mid) + _dot(tri, lo)
    negs, b_last = _level_neg_exponents(b, c, nch)
    km = kmask.astype(BF16)

    def stack_heads(x_bf16):
        return jnp.concatenate([x_bf16] * GLA_HEADS, axis=0) * km

    q_bf, k_bf = q.astype(BF16), k.astype(BF16)
    att = [_dot_nt(rows(q_bf, ci), stack_heads(rows(k_bf, ci))) * amask[len(levels)] for ci in range(nch)]
    for l, m in enumerate(levels):
        w = jnp.exp2(negs[m])
        qw, kw = (q * w).astype(BF16), (k * w).astype(BF16)
        for ci in range(nch):
            att[ci] = att[ci] + _dot_nt(rows(qw, ci), stack_heads(rows(kw, ci))) * amask[l]
    b3 = _tile3(b)
    brev = (jnp.stack([b_last[j // g] for j in range(nch * g)]) - b3).reshape(b.shape)
    ke = (k * jnp.exp2(brev)).astype(BF16)
    qb = (q * jnp.exp2(b)).astype(BF16)
    v_bf = v.astype(BF16)
    v_heads = [[rows(v_bf, ci)[:, h * GLA_DV:(h + 1) * GLA_DV] for h in range(GLA_HEADS)] for ci in range(nch)]
    upd = [_dot_tn(jnp.concatenate(v_heads[ci], axis=0), stack_heads(rows(ke, ci))) for ci in range(nch)]
    st_at, st_out = [], []
    for si in range(ns):
        st = sts[si]
        for ci in range(si * nc, (si + 1) * nc):
            st_at.append(st)
            st = st * jnp.exp2(b_last[ci][0:1, :]) + upd[ci]
        st_out.append(st)
    outs = []
    for ci in range(nch):
        att_bf = att[ci].astype(BF16)
        o_intra = [_dot(att_bf[:, h * c:(h + 1) * c], v_heads[ci][h]) for h in range(GLA_HEADS)]
        o_inter = _dot_nt(stack_heads(rows(qb, ci)), st_at[ci].astype(BF16))
        outs.append(o_inter + jnp.concatenate(o_intra, axis=0))
    return outs, st_out


def _gla_finish(outs, g, gnorm, c):
    nch = len(outs)
    o = jnp.concatenate(outs, axis=0)
    o = o * lax.rsqrt(jnp.mean(o * o, axis=-1, keepdims=True) + EPS) * gnorm
    o = jnp.concatenate([jnp.concatenate([o[(ci * GLA_HEADS + h) * c:(ci * GLA_HEADS + h + 1) * c]
                                          for h in range(GLA_HEADS)], axis=1) for ci in range(nch)], axis=0)
    return o * (g * jax.nn.sigmoid(g))


def _rg_block(xr, gr, tails, carries, cw, cb, wrg, ba, bx, lam, ns, tb):
    ng = tb // SUBLANES
    x3 = _tile3(xr)
    sub = lax.broadcasted_iota(jnp.int32, x3.shape, 1)
    tail3 = jnp.stack(tails)
    xc = cb + x3 * cw[CONV_W - 1:CONV_W, :]
    for s in range(1, CONV_W):
        rolled = pltpu.roll(x3, s, axis=1)
        rolled_tail = pltpu.roll(tail3, s, axis=1)
        prev = jnp.stack([rolled_tail[j // ng] if j % ng == 0 else rolled[j - 1] for j in range(ns * ng)])
        xc = xc + jnp.where(sub >= s, rolled, prev) * cw[CONV_W - 1 - s:CONV_W - s, :]
    gates = _tile3(_dot(xc.reshape(ns * tb, RG_WIDTH).astype(BF16), wrg))
    r = jax.nn.sigmoid(gates[:, :, :RG_WIDTH] + ba)
    i = jax.nn.sigmoid(gates[:, :, RG_WIDTH:] + bx)
    softplus = jnp.maximum(-lam, 0.0) + jnp.log1p(jnp.exp(-jnp.abs(lam)))
    log_a = -RG_C * r * softplus
    a = jnp.exp(log_a)
    e = (1.0 - a) * (1.0 + a)
    u = jnp.where(e > 0.0, e * lax.rsqrt(e), 0.0) * (i * xc)

    s = 1
    while s < SUBLANES:
        keep = sub >= s
        u = jnp.where(keep, a * pltpu.roll(u, s, axis=1) + u, u)
        a = jnp.where(keep, a * pltpu.roll(a, s, axis=1), a)
        s *= 2
    hs, new_carries = [], []
    for si in range(ns):
        carry = carries[si]
        for gi in range(si * ng, (si + 1) * ng):
            hg = a[gi] * carry + u[gi]
            carry = hg[SUBLANES - 1:SUBLANES, :]
            hs.append(hg)
        new_carries.append(carry)
    hr = jnp.stack(hs).reshape(ns * tb, RG_WIDTH)
    new_tails = [x3[(si + 1) * ng - 1] for si in range(ns)]
    return hr * jax.nn.gelu(gr), new_tails, new_carries


def _mixer_kernel(qk_ref, v_ref, gg_ref, la_ref, xr_ref, gr_ref,
                  st0_ref, h0_ref, buf0_ref,
                  gnorm_ref, convw_ref, convb_ref, wrg_ref, ba_ref, bx_ref, lam_ref,
                  tri_ref, amask_ref, kmask_ref,
                  mix_ref, st_out_ref, h_out_ref, buf_out_ref,
                  st_sc, h_sc, tail_sc, *, ns, tb, c):
    t = pl.program_id(1)
    nt = pl.num_programs(1)
    nc = tb // c
    flat = lambda ref: ref[...].reshape(ns * tb, ref.shape[-1])

    @pl.when(t == 0)
    def _():
        st_sc[...] = st0_ref[...]
        h_sc[...] = h0_ref[...]
        tail_sc[...] = buf0_ref[...]

    yr, tails, carries = _rg_block(flat(xr_ref), flat(gr_ref),
                                   [tail_sc[si] for si in range(ns)], [h_sc[si] for si in range(ns)],
                                   convw_ref[...], convb_ref[...], wrg_ref[...], ba_ref[...], bx_ref[...],
                                   lam_ref[...], ns, tb)
    mix_ref[:, :, GLA_WIDTH:] = yr.reshape(ns, tb, RG_WIDTH).astype(mix_ref.dtype)
    for si in range(ns):
        tail_sc[si] = tails[si]
        h_sc[si] = carries[si]

    qk = flat(qk_ref)
    outs, sts = _gla_block(qk[:, 0:GLA_QK], qk[:, GLA_QK:], flat(v_ref), flat(la_ref),
                           [st_sc[si] for si in range(ns)], tri_ref[...], amask_ref[...], kmask_ref[...],
                           c, ns, nc)
    o = _gla_finish(outs, flat(gg_ref), gnorm_ref[...], c)
    mix_ref[:, :, 0:GLA_WIDTH] = o.reshape(ns, tb, GLA_WIDTH).astype(mix_ref.dtype)
    for si in range(ns):
        st_sc[si] = sts[si]

    @pl.when(t == nt - 1)
    def _():
        for si in range(ns):
            st_out_ref[si] = sts[si]
            h_out_ref[si] = carries[si]
            buf_out_ref[si] = tails[si]


def _mixer(parts, st0, h0, buf0, prm, ns, tb, c):
    qk = parts[0]
    nb, tt = qk.shape[0], qk.shape[1]
    tri, amask, kmask = _mixer_consts(c)
    tri = np.kron(np.eye(ns * tb // c, dtype=np.float32), tri)
    seq = lambda w: pl.BlockSpec((ns, tb, w), lambda b, t: (b, t, 0))
    per_b = lambda s1, s2: pl.BlockSpec((ns, s1, s2), lambda b, t: (b, 0, 0))
    const2 = lambda shape: pl.BlockSpec(shape, lambda b, t: (0,) * len(shape))
    kern = functools.partial(_mixer_kernel, ns=ns, tb=tb, c=c)
    return pl.pallas_call(
        kern,
        grid=(nb // ns, tt // tb),
        in_specs=[seq(2 * GLA_QK), seq(GLA_WIDTH), seq(GLA_WIDTH), seq(GLA_QK), seq(RG_WIDTH), seq(RG_WIDTH),
                  per_b(GLA_DV, GLA_QK), per_b(1, RG_WIDTH), per_b(SUBLANES, RG_WIDTH),
                  const2((1, GLA_DV)), const2((SUBLANES, RG_WIDTH)), const2((1, RG_WIDTH)),
                  const2((RG_WIDTH, 2 * RG_WIDTH)), const2((1, RG_WIDTH)), const2((1, RG_WIDTH)),
                  const2((1, RG_WIDTH)),
                  const2(tri.shape), const2(amask.shape), const2(kmask.shape)],
        out_specs=[seq(D_MODEL), per_b(GLA_DV, GLA_QK), per_b(1, RG_WIDTH), per_b(SUBLANES, RG_WIDTH)],
        out_shape=[jax.ShapeDtypeStruct((nb, tt, D_MODEL), BF16),
                   jax.ShapeDtypeStruct((nb, GLA_DV, GLA_QK), F32),
                   jax.ShapeDtypeStruct((nb, 1, RG_WIDTH), F32),
                   jax.ShapeDtypeStruct((nb, SUBLANES, RG_WIDTH), F32)],
        scratch_shapes=[pltpu.VMEM((ns, GLA_DV, GLA_QK), F32),
                        pltpu.VMEM((ns, 1, RG_WIDTH), F32),
                        pltpu.VMEM((ns, SUBLANES, RG_WIDTH), F32)],
        compiler_params=pltpu.CompilerParams(dimension_semantics=("arbitrary", "arbitrary"),
                                             vmem_limit_bytes=VMEM_LIMIT),
        name="mixer_c%d" % c,
    )(*parts, st0, h0, buf0, prm["gla_norm"], prm["conv_w"], prm["conv_b"], prm["w_rg"],
      prm["rg_ba"], prm["rg_bx"], prm["rg_lambda"],
      jnp.asarray(tri, BF16), jnp.asarray(amask, F32), jnp.asarray(kmask, F32))


FF_TILE = 1024


def _out_ffn_kernel(x_ref, mix_ref, wout_ref, gpm_ref, gpf_ref, w1_ref, w2_ref, gpo_ref, o_ref):
    mo = _dot(mix_ref[...], wout_ref[...])
    x1 = x_ref[...] + _rms(mo, gpm_ref[...])
    xn = _rms(x1, gpf_ref[...]).astype(BF16)
    f = jnp.zeros(x1.shape, F32)
    for j in range(D_FF // FF_TILE):
        cols = slice(j * FF_TILE, (j + 1) * FF_TILE)
        hdn = jnp.square(jnp.maximum(_dot(xn, w1_ref[:, cols]), 0.0)).astype(BF16)
        f = f + _dot(hdn, w2_ref[cols, :])
    o_ref[...] = x1 + _rms(f, gpo_ref[...])


def _out_ffn(x2d, mix2d, prm, tm):
    n = x2d.shape[0]
    const = lambda i: (0, 0)
    row = lambda i: (i, 0)
    wspec = lambda shape: pl.BlockSpec(shape, const, pipeline_mode=pl.Buffered(1))
    return pl.pallas_call(
        _out_ffn_kernel,
        grid=(n // tm,),
        in_specs=[pl.BlockSpec((tm, D_MODEL), row),
                  pl.BlockSpec((tm, D_MODEL), row),
                  wspec((D_MODEL, D_MODEL)), wspec((1, D_MODEL)), wspec((1, D_MODEL)),
                  wspec((D_MODEL, D_FF)), wspec((D_FF, D_MODEL)), wspec((1, D_MODEL))],
        out_specs=pl.BlockSpec((tm, D_MODEL), row),
        out_shape=jax.ShapeDtypeStruct((n, D_MODEL), F32),
        compiler_params=pltpu.CompilerParams(dimension_semantics=("arbitrary",),
                                             vmem_limit_bytes=VMEM_LIMIT),
        name="out_ffn",
    )(x2d, mix2d, prm["w_out"], prm["g_post_mix"], prm["g_pre_ff"], prm["w_ff1"], prm["w_ff2"],
      prm["g_post_ff"])


def _prep_layer(l, g_pre_mix, w_in, w_lr2, b_lr, gla_norm, conv_w, conv_b, rg_wa, rg_ba, rg_wx,
                rg_bx, rg_lambda, w_out, g_post_mix, g_pre_ff, w_ff1, w_ff2, g_post_ff):
    off_lr = 2 * GLA_QK + 2 * GLA_WIDTH
    off_xr = off_lr + GLA_RANK
    w = w_in[l]
    w_main = jnp.concatenate([w[:, :off_lr], w[:, off_xr:]], axis=1).astype(BF16)
    w_lr1 = jnp.pad(w[:, off_lr:off_xr], ((0, 0), (0, LR_PAD - GLA_RANK))).astype(BF16)
    w_lr2p = jnp.pad(w_lr2[l], ((0, LR_PAD - GLA_RANK), (0, 0))).astype(BF16)
    eye = jnp.eye(RG_BLOCKS, dtype=F32)
    bd = lambda wb: jnp.einsum("gij,gh->gihj", wb, eye).reshape(RG_WIDTH, RG_WIDTH)
    w_rg = jnp.concatenate([bd(rg_wa[l]), bd(rg_wx[l])], axis=1).astype(BF16)
    row = lambda a: a[l].reshape(1, -1)
    return dict(
        g_pre_mix=row(g_pre_mix), w_main=w_main, w_lr1=w_lr1, w_lr2=w_lr2p, b_lr=row(b_lr),
        gla_norm=row(gla_norm),
        conv_w=jnp.pad(conv_w[l], ((0, SUBLANES - CONV_W), (0, 0))), conv_b=row(conv_b),
        w_rg=w_rg, rg_ba=row(rg_ba), rg_bx=row(rg_bx), rg_lambda=row(rg_lambda),
        w_out=w_out[l].astype(BF16), g_post_mix=row(g_post_mix), g_pre_ff=row(g_pre_ff),
        w_ff1=w_ff1[l].astype(BF16), w_ff2=w_ff2[l].astype(BF16), g_post_ff=row(g_post_ff))


def _layer(x, st0, h0, buf0, prm, ns, tb, c, tm):
    nb, tt, _ = x.shape
    x2d = x.reshape(nb * tt, D_MODEL)
    parts = _in_proj(x2d, prm["g_pre_mix"], prm["w_main"], prm["w_lr1"], prm["w_lr2"], prm["b_lr"], tm)
    parts = [p.reshape(nb, tt, p.shape[-1]) for p in parts]
    mix, st, h, buf = _mixer(parts, st0, h0, buf0, prm, ns, tb, c)
    x_new = _out_ffn(x2d, mix.reshape(nb * tt, D_MODEL), prm, tm)
    return x_new.reshape(nb, tt, D_MODEL), st, h, buf


def _state_in(s):
    return jnp.transpose(s, (0, 3, 1, 2)).reshape(s.shape[0], GLA_DV, GLA_QK)


def _state_out(st):
    return jnp.transpose(st.reshape(st.shape[0], GLA_DV, GLA_HEADS, GLA_DK), (0, 2, 3, 1))


def kernel(x_prompt, x_sample, state_gla, state_rglru, state_conv, g_pre_mix, w_in, w_lr2, b_lr, gla_norm, conv_w, conv_b, rg_wa, rg_ba, rg_wx, rg_bx, rg_lambda, w_out, g_post_mix, g_pre_ff, w_ff1, w_ff2, g_post_ff):
    depth = w_in.shape[0]
    bp, tp, _ = x_prompt.shape
    bs, ts, _ = x_sample.shape
    cp, cs = min(CHUNK, tp), min(CHUNK, ts)
    tbp, tbs = min(256, tp), min(256, ts)
    nss = max(1, min(bs, 256 // tbs))
    pad_buf = ((0, 0), (SUBLANES - (CONV_W - 1), 0), (0, 0))
    xp, xs = x_prompt, x_sample
    outs = [[] for _ in range(6)]
    for l in range(depth):
        prm = _prep_layer(l, g_pre_mix, w_in, w_lr2, b_lr, gla_norm, conv_w, conv_b, rg_wa, rg_ba,
                          rg_wx, rg_bx, rg_lambda, w_out, g_post_mix, g_pre_ff, w_ff1, w_ff2, g_post_ff)
        xp, st, h, buf = _layer(xp,
                                jnp.zeros((bp, GLA_DV, GLA_QK), F32),
                                jnp.zeros((bp, 1, RG_WIDTH), F32),
                                jnp.zeros((bp, SUBLANES, RG_WIDTH), F32),
                                prm, 1, tbp, cp, 512)
        outs[0].append(_state_out(st))
        outs[1].append(h[:, 0, :])
        outs[2].append(buf[:, SUBLANES - (CONV_W - 1):, :])
        xs, st, h, buf = _layer(xs,
                                _state_in(state_gla[l]),
                                state_rglru[l][:, None, :],
                                jnp.pad(state_conv[l], pad_buf),
                                prm, nss, tbs, cs, 512)
        outs[3].append(_state_out(st))
        outs[4].append(h[:, 0, :])
        outs[5].append(buf[:, SUBLANES - (CONV_W - 1):, :])
    return (xp, xs) + tuple(jnp.stack(o) for o in outs)
```

```python
import functools

import numpy as np
import jax
import jax.numpy as jnp
from jax import lax
from jax.experimental import pallas as pl
from jax.experimental.pallas import tpu as pltpu

F32 = jnp.float32
BF16 = jnp.bfloat16

D_MODEL = 1024
GLA_HEADS = 4
GLA_DK = 64
GLA_DV = 128
GLA_QK = GLA_HEADS * GLA_DK
GLA_WIDTH = GLA_HEADS * GLA_DV
GLA_RANK = 16
GLA_TAU = 16.0
RG_WIDTH = D_MODEL - GLA_WIDTH
RG_BLOCKS = 8
RG_BLOCK = RG_WIDTH // RG_BLOCKS
RG_C = 8.0
CONV_W = 4
D_FF = 4 * D_MODEL
EPS = 1e-6
CHUNK = 64
LOG2E = 1.4426950408889634

LANES = 128
SUBLANES = 8
LR_PAD = LANES
Z_MAIN = 2 * GLA_QK + 2 * GLA_WIDTH + 2 * RG_WIDTH
VMEM_LIMIT = 56 * 1024 * 1024


def _rms(x, g):
    return x * lax.rsqrt(jnp.mean(x * x, axis=-1, keepdims=True) + EPS) * g


def _dot(a, b):
    return jnp.dot(a, b, preferred_element_type=F32)


def _dot_nt(a, b):
    return lax.dot_general(a, b, (((1,), (1,)), ((), ())), preferred_element_type=F32)


def _dot_tn(a, b):
    return lax.dot_general(a, b, (((0,), (0,)), ((), ())), preferred_element_type=F32)


def _split3(x):
    hi = x.astype(BF16)
    r = x - hi.astype(F32)
    mid = r.astype(BF16)
    lo = (r - mid.astype(F32)).astype(BF16)
    return hi, mid, lo


def _levels(c):
    m, out = c // 2, []
    while m >= 1:
        out.append(m)
        m //= 2
    return out


@functools.lru_cache(maxsize=None)
def _mixer_consts(c):
    t = np.arange(c)
    tri = (t[None, :] <= t[:, None]).astype(np.float32)
    masks = []
    for m in _levels(c):
        same = (t[:, None] // (2 * m)) == (t[None, :] // (2 * m))
        odd_i = ((t[:, None] // m) % 2) == 1
        even_j = ((t[None, :] // m) % 2) == 0
        masks.append((same & odd_i & even_j).astype(np.float32))
    masks.append(np.eye(c, dtype=np.float32))
    amask = np.stack([np.tile(mk, (1, GLA_HEADS)) for mk in masks])
    hrow = np.repeat(np.arange(GLA_HEADS), c)
    hcol = np.repeat(np.arange(GLA_HEADS), GLA_DK)
    kmask = (hrow[:, None] == hcol[None, :]).astype(np.float32)
    return tri, amask, kmask


def _tile3(x):
    return x.reshape(x.shape[0] // SUBLANES, SUBLANES, x.shape[1])


def _level_neg_exponents(b, c, nch):
    g = c // SUBLANES
    b3 = _tile3(b)
    sub = lax.broadcasted_iota(jnp.int32, b3.shape, 1)
    sub1 = lax.broadcasted_iota(jnp.int32, (1,) + b3.shape[1:], 1)
    ends = {1: b3}
    cur = b3
    for m in (2, 4, 8):
        half = m // 2
        cur = jnp.where((sub & half) != 0, cur, pltpu.roll(cur, SUBLANES - half, axis=1))
        ends[m] = cur
    tile_end = [cur[j] for j in range(nch * g)]
    out = {}
    for m in _levels(c):
        if m < SUBLANES:
            e = ends[m]
            ref = jnp.where((sub & m) != 0, pltpu.roll(e, m, axis=1), e)
            sign = jnp.where((sub1 & m) != 0, 1.0, -1.0)
            neg = (b3 - ref) * sign
        else:
            tm = m // SUBLANES
            tiles = []
            for j in range(nch * g):
                jl = j % g
                ref = tile_end[j - jl + (jl // (2 * tm)) * (2 * tm) + tm - 1]
                tiles.append(b3[j] - ref if (jl // tm) % 2 == 1 else ref - b3[j])
            neg = jnp.stack(tiles)
        out[m] = neg.reshape(b.shape)
    return out, [tile_end[ci * g + g - 1] for ci in range(nch)]


def _gla_phases(q, k, v, la2, sts, tri, amask, kmask, c, ns, nc, res):
    levels = _levels(c)
    nch = ns * nc
    g = c // SUBLANES
    rows = lambda x, ci: x[ci * c:(ci + 1) * c]
    hi, mid, lo = _split3(la2)
    b = _dot(tri, hi) + _dot(tri, mid) + _dot(tri, lo)
    negs, b_last = _level_neg_exponents(b, c, nch)
    km = kmask.astype(BF16)

    def stack_heads(x_bf16):
        return jnp.concatenate([x_bf16] * GLA_HEADS, axis=0) * km

    q_bf, k_bf = q.astype(BF16), k.astype(BF16)
    att = [_dot_nt(rows(q_bf, ci), stack_heads(rows(k_bf, ci))) * amask[len(levels)] for ci in range(nch)]
    yield
    for l, m in enumerate(levels):
        w = jnp.exp2(negs[m])
        qw, kw = (q * w).astype(BF16), (k * w).astype(BF16)
        for ci in range(nch):
            att[ci] = att[ci] + _dot_nt(rows(qw, ci), stack_heads(rows(kw, ci))) * amask[l]
        if l % 2 == 1:
            yield
    b3 = _tile3(b)
    brev = (jnp.stack([b_last[j // g] for j in range(nch * g)]) - b3).reshape(b.shape)
    ke = (k * jnp.exp2(brev)).astype(BF16)
    qb = (q * jnp.exp2(b)).astype(BF16)
    v_bf = v.astype(BF16)
    v_heads = [[rows(v_bf, ci)[:, h * GLA_DV:(h + 1) * GLA_DV] for h in range(GLA_HEADS)] for ci in range(nch)]
    upd = [_dot_tn(jnp.concatenate(v_heads[ci], axis=0), stack_heads(rows(ke, ci))) for ci in range(nch)]
    st_at, st_out = [], []
    for si in range(ns):
        st = sts[si]
        for ci in range(si * nc, (si + 1) * nc):
            st_at.append(st)
            st = st * jnp.exp2(b_last[ci][0:1, :]) + upd[ci]
        st_out.append(st)
    yield
    outs = []
    for ci in range(nch):
        att_bf = att[ci].astype(BF16)
        o_intra = [_dot(att_bf[:, h * c:(h + 1) * c], v_heads[ci][h]) for h in range(GLA_HEADS)]
        o_inter = _dot_nt(stack_heads(rows(qb, ci)), st_at[ci].astype(BF16))
        outs.append(o_inter + jnp.concatenate(o_intra, axis=0))
    res["outs"], res["sts"] = outs, st_out


def _gla_finish(outs, g, gnorm, c):
    nch = len(outs)
    o = jnp.concatenate(outs, axis=0)
    o = o * lax.rsqrt(jnp.mean(o * o, axis=-1, keepdims=True) + EPS) * gnorm
    o = jnp.concatenate([jnp.concatenate([o[(ci * GLA_HEADS + h) * c:(ci * GLA_HEADS + h + 1) * c]
                                          for h in range(GLA_HEADS)], axis=1) for ci in range(nch)], axis=0)
    return o * (g * jax.nn.sigmoid(g))


def _rg_phases(xr, gr, tails, carries, cw, cb, wrg, ba, bx, lam, ns, tb, res):
    ng = tb // SUBLANES
    x3 = _tile3(xr)
    sub = lax.broadcasted_iota(jnp.int32, x3.shape, 1)
    tail3 = jnp.stack(tails)
    xc = cb + x3 * cw[CONV_W - 1:CONV_W, :]
    for s in range(1, CONV_W):
        rolled = pltpu.roll(x3, s, axis=1)
        rolled_tail = pltpu.roll(tail3, s, axis=1)
        prev = jnp.stack([rolled_tail[j // ng] if j % ng == 0 else rolled[j - 1] for j in range(ns * ng)])
        xc = xc + jnp.where(sub >= s, rolled, prev) * cw[CONV_W - 1 - s:CONV_W - s, :]
    gates = _tile3(_dot(xc.reshape(ns * tb, RG_WIDTH).astype(BF16), wrg))
    yield
    r = jax.nn.sigmoid(gates[:, :, :RG_WIDTH] + ba)
    i = jax.nn.sigmoid(gates[:, :, RG_WIDTH:] + bx)
    softplus = jnp.maximum(-lam, 0.0) + jnp.log1p(jnp.exp(-jnp.abs(lam)))
    log_a = -RG_C * r * softplus
    a = jnp.exp(log_a)
    e = (1.0 - a) * (1.0 + a)
    u = jnp.where(e > 0.0, e * lax.rsqrt(e), 0.0) * (i * xc)
    yield
    s = 1
    while s < SUBLANES:
        keep = sub >= s
        u = jnp.where(keep, a * pltpu.roll(u, s, axis=1) + u, u)
        a = jnp.where(keep, a * pltpu.roll(a, s, axis=1), a)
        s *= 2
    yield
    hs, new_carries = [], []
    for si in range(ns):
        carry = carries[si]
        for gi in range(si * ng, (si + 1) * ng):
            hg = a[gi] * carry + u[gi]
            carry = hg[SUBLANES - 1:SUBLANES, :]
            hs.append(hg)
        new_carries.append(carry)
    hr = jnp.stack(hs).reshape(ns * tb, RG_WIDTH)
    res["yr"] = hr * jax.nn.gelu(gr)
    res["tails"] = [x3[(si + 1) * ng - 1] for si in range(ns)]
    res["carries"] = new_carries


FF_TILE = 1024
PHASE_ORDER = "BAABAABAABAABAABAA"


def _layer_kernel(xa_ref, xb_ref, st0_ref, h0_ref, buf0_ref,
                  gpre_ref, win_ref, wlr1_ref, wlr2_ref, blr_ref,
                  gnorm_ref, convw_ref, convb_ref, wrg_ref, ba_ref, bx_ref, lam_ref,
                  wout_ref, gpm_ref, gpf_ref, w1_ref, w2_ref, gpo_ref,
                  tri_ref, amask_ref, kmask_ref,
                  y_ref, st_out_ref, h_out_ref, buf_out_ref,
                  st_sc, h_sc, tail_sc, mix_sc, *, ns, tb, c, nt, nblk):
    s = pl.program_id(0)
    t = jnp.minimum(s, nblk - 1) % nt
    nc = tb // c
    rows = ns * tb

    @pl.when(s == 0)
    def _():
        mix_sc[...] = jnp.zeros(mix_sc.shape, mix_sc.dtype)

    @pl.when(t == 0)
    def _():
        st_sc[...] = st0_ref[...]
        h_sc[...] = h0_ref[...]
        tail_sc[...] = buf0_ref[...]

    res = {}

    def stage_a():
        h = _rms(xa_ref[...].reshape(rows, D_MODEL), gpre_ref[...]).astype(BF16)
        z = _dot(h, win_ref[...])
        lr = _dot(h, wlr1_ref[...])
        yield
        pre = _dot(lr.astype(BF16), wlr2_ref[...]) + blr_ref[...]
        la2 = (jnp.minimum(pre, 0.0) - jnp.log1p(jnp.exp(-jnp.abs(pre)))) * (LOG2E / GLA_TAU)
        o = 0
        q = z[:, o:o + GLA_QK] * (GLA_DK ** -0.5)
        k = z[:, o + GLA_QK:o + 2 * GLA_QK]
        o += 2 * GLA_QK
        v = z[:, o:o + GLA_WIDTH]
        o += GLA_WIDTH
        g = z[:, o:o + GLA_WIDTH]
        o += GLA_WIDTH
        xr = z[:, o:o + RG_WIDTH]
        o += RG_WIDTH
        gr = z[:, o:o + RG_WIDTH]
        yield from _rg_phases(xr, gr, [tail_sc[si] for si in range(ns)], [h_sc[si] for si in range(ns)],
                              convw_ref[...], convb_ref[...], wrg_ref[...], ba_ref[...], bx_ref[...],
                              lam_ref[...], ns, tb, res)
        yield
        yield from _gla_phases(q, k, v, la2, [st_sc[si] for si in range(ns)],
                               tri_ref[...], amask_ref[...], kmask_ref[...], c, ns, nc, res)
        yield
        og = _gla_finish(res["outs"], g, gnorm_ref[...], c)
        res["mix"] = jnp.concatenate([og, res["yr"]], axis=1).astype(mix_sc.dtype)

    def stage_b():
        mo = _dot(mix_sc[...], wout_ref[...])
        yield
        x1 = xb_ref[...].reshape(rows, D_MODEL) + _rms(mo, gpm_ref[...])
        xn = _rms(x1, gpf_ref[...]).astype(BF16)
        f = jnp.zeros(x1.shape, F32)
        for j in range(D_FF // FF_TILE):
            cols = slice(j * FF_TILE, (j + 1) * FF_TILE)
            hdn = jnp.square(jnp.maximum(_dot(xn, w1_ref[:, cols]), 0.0)).astype(BF16)
            f = f + _dot(hdn, w2_ref[cols, :])
            yield
        y_ref[...] = (x1 + _rms(f, gpo_ref[...])).reshape(y_ref.shape)

    gens = {"A": stage_a(), "B": stage_b()}
    for p in PHASE_ORDER:
        next(gens[p], None)
    for gen in gens.values():
        for _ in gen:
            pass

    mix_sc[...] = res["mix"]
    for si in range(ns):
        tail_sc[si] = res["tails"][si]
        h_sc[si] = res["carries"][si]
        st_sc[si] = res["sts"][si]

    @pl.when(jnp.logical_and(t == nt - 1, s < nblk))
    def _():
        for si in range(ns):
            st_out_ref[si] = res["sts"][si]
            h_out_ref[si] = res["carries"][si]
            buf_out_ref[si] = res["tails"][si]


def _layer(x, st0, h0, buf0, prm, ns, tb, c):
    nb, tt, _ = x.shape
    nt = tt // tb
    nblk = (nb // ns) * nt
    tri, amask, kmask = _mixer_consts(c)
    tri = np.kron(np.eye(ns * tb // c, dtype=np.float32), tri)
    blk_a = lambda s: jnp.minimum(s, nblk - 1)
    blk_b = lambda s: jnp.maximum(s - 1, 0)
    xspec = lambda blk: pl.BlockSpec((ns, tb, D_MODEL), lambda s: (blk(s) // nt, blk(s) % nt, 0))
    per_b = lambda s1, s2: pl.BlockSpec((ns, s1, s2), lambda s: (blk_a(s) // nt, 0, 0))
    wspec = lambda a: pl.BlockSpec(a.shape, lambda s: (0,) * a.ndim, pipeline_mode=pl.Buffered(1))
    weights = [prm[n] for n in ("g_pre_mix", "w_main", "w_lr1", "w_lr2", "b_lr",
                                "gla_norm", "conv_w", "conv_b", "w_rg", "rg_ba", "rg_bx", "rg_lambda",
                                "w_out", "g_post_mix", "g_pre_ff", "w_ff1", "w_ff2", "g_post_ff")]
    weights += [jnp.asarray(tri, BF16), jnp.asarray(amask, F32), jnp.asarray(kmask, F32)]
    kern = functools.partial(_layer_kernel, ns=ns, tb=tb, c=c, nt=nt, nblk=nblk)
    return pl.pallas_call(
        kern,
        grid=(nblk + 1,),
        in_specs=[xspec(blk_a), xspec(blk_b),
                  per_b(GLA_DV, GLA_QK), per_b(1, RG_WIDTH), per_b(SUBLANES, RG_WIDTH)]
                 + [wspec(w) for w in weights],
        out_specs=[xspec(blk_b), per_b(GLA_DV, GLA_QK), per_b(1, RG_WIDTH), per_b(SUBLANES, RG_WIDTH)],
        out_shape=[jax.ShapeDtypeStruct((nb, tt, D_MODEL), F32),
                   jax.ShapeDtypeStruct((nb, GLA_DV, GLA_QK), F32),
                   jax.ShapeDtypeStruct((nb, 1, RG_WIDTH), F32),
                   jax.ShapeDtypeStruct((nb, SUBLANES, RG_WIDTH), F32)],
        scratch_shapes=[pltpu.VMEM((ns, GLA_DV, GLA_QK), F32),
                        pltpu.VMEM((ns, 1, RG_WIDTH), F32),
                        pltpu.VMEM((ns, SUBLANES, RG_WIDTH), F32),
                        pltpu.VMEM((ns * tb, D_MODEL), BF16)],
        compiler_params=pltpu.CompilerParams(dimension_semantics=("arbitrary",),
                                             vmem_limit_bytes=VMEM_LIMIT),
        name="layer_c%d" % c,
    )(x, x, st0, h0, buf0, *weights)


def _prep_layer(l, g_pre_mix, w_in, w_lr2, b_lr, gla_norm, conv_w, conv_b, rg_wa, rg_ba, rg_wx,
                rg_bx, rg_lambda, w_out, g_post_mix, g_pre_ff, w_ff1, w_ff2, g_post_ff):
    off_lr = 2 * GLA_QK + 2 * GLA_WIDTH
    off_xr = off_lr + GLA_RANK
    w = w_in[l]
    w_main = jnp.concatenate([w[:, :off_lr], w[:, off_xr:]], axis=1).astype(BF16)
    w_lr1 = jnp.pad(w[:, off_lr:off_xr], ((0, 0), (0, LR_PAD - GLA_RANK))).astype(BF16)
    w_lr2p = jnp.pad(w_lr2[l], ((0, LR_PAD - GLA_RANK), (0, 0))).astype(BF16)
    eye = jnp.eye(RG_BLOCKS, dtype=F32)
    bd = lambda wb: jnp.einsum("gij,gh->gihj", wb, eye).reshape(RG_WIDTH, RG_WIDTH)
    w_rg = jnp.concatenate([bd(rg_wa[l]), bd(rg_wx[l])], axis=1).astype(BF16)
    row = lambda a: a[l].reshape(1, -1)
    return dict(
        g_pre_mix=row(g_pre_mix), w_main=w_main, w_lr1=w_lr1, w_lr2=w_lr2p, b_lr=row(b_lr),
        gla_norm=row(gla_norm),
        conv_w=jnp.pad(conv_w[l], ((0, SUBLANES - CONV_W), (0, 0))), conv_b=row(conv_b),
        w_rg=w_rg, rg_ba=row(rg_ba), rg_bx=row(rg_bx), rg_lambda=row(rg_lambda),
        w_out=w_out[l].astype(BF16), g_post_mix=row(g_post_mix), g_pre_ff=row(g_pre_ff),
        w_ff1=w_ff1[l].astype(BF16), w_ff2=w_ff2[l].astype(BF16), g_post_ff=row(g_post_ff))


def _state_in(s):
    return jnp.transpose(s, (0, 3, 1, 2)).reshape(s.shape[0], GLA_DV, GLA_QK)


def _state_out(st):
    return jnp.transpose(st.reshape(st.shape[0], GLA_DV, GLA_HEADS, GLA_DK), (0, 2, 3, 1))


def kernel(x_prompt, x_sample, state_gla, state_rglru, state_conv, g_pre_mix, w_in, w_lr2, b_lr, gla_norm, conv_w, conv_b, rg_wa, rg_ba, rg_wx, rg_bx, rg_lambda, w_out, g_post_mix, g_pre_ff, w_ff1, w_ff2, g_post_ff):
    depth = w_in.shape[0]
    bp, tp, _ = x_prompt.shape
    bs, ts, _ = x_sample.shape
    cp, cs = min(CHUNK, tp), min(CHUNK, ts)
    tbp, tbs = min(256, tp), min(256, ts)
    nss = max(1, min(bs, 256 // tbs))
    pad_buf = ((0, 0), (SUBLANES - (CONV_W - 1), 0), (0, 0))
    xp, xs = x_prompt, x_sample
    outs = [[] for _ in range(6)]
    for l in range(depth):
        prm = _prep_layer(l, g_pre_mix, w_in, w_lr2, b_lr, gla_norm, conv_w, conv_b, rg_wa, rg_ba,
                          rg_wx, rg_bx, rg_lambda, w_out, g_post_mix, g_pre_ff, w_ff1, w_ff2, g_post_ff)
        xp, st, h, buf = _layer(xp,
                                jnp.zeros((bp, GLA_DV, GLA_QK), F32),
                                jnp.zeros((bp, 1, RG_WIDTH), F32),
                                jnp.zeros((bp, SUBLANES, RG_WIDTH), F32),
                                prm, 1, tbp, cp)
        outs[0].append(_state_out(st))
        outs[1].append(h[:, 0, :])
        outs[2].append(buf[:, SUBLANES - (CONV_W - 1):, :])
        xs, st, h, buf = _layer(xs,
                                _state_in(state_gla[l]),
                                state_rglru[l][:, None, :],
                                jnp.pad(state_conv[l], pad_buf),
                                prm, nss, tbs, cs)
        outs[3].append(_state_out(st))
        outs[4].append(h[:, 0, :])
        outs[5].append(buf[:, SUBLANES - (CONV_W - 1):, :])
    return (xp, xs) + tuple(jnp.stack(o) for o in outs)
```
